```python
import jax, jax.numpy as jnp
from jax import lax
import numpy as np

D_MODEL = 1024
BATCH = 8
SEQ = 4096
DEPTH = 1

PLE_DIM = 256
ATTN_HEADS = 8
ATTN_KV_HEADS = 2
ATTN_HEAD_DIM = 64
ATTN_GROUPS = ATTN_HEADS // ATTN_KV_HEADS
WINDOW = 128
ATTN_BLOCK = 128
ROPE_THETA = 10000.0
DN_HEADS = 4
DN_HEAD_DIM = 128
DN_CONV = 4
DN_CHUNK = 64
D_FF = 4 * D_MODEL
EPS = 1e-6

ATTN_Q = ATTN_HEADS * ATTN_HEAD_DIM
ATTN_KV = ATTN_KV_HEADS * ATTN_HEAD_DIM
DN_W = DN_HEADS * DN_HEAD_DIM
MIX_WIDTH = ATTN_Q + DN_W
SPLIT_SIZES = (ATTN_Q, ATTN_KV, ATTN_KV, DN_W, DN_W, DN_W, DN_W, DN_HEADS, DN_HEADS)
D_IN = sum(SPLIT_SIZES)
CONV_CH = 3 * DN_W

kernel_name = "hybrid_swa_sink_gated_deltanet_block"


def rmsnorm(x, g):
    xf = x.astype(jnp.float32)
    y = xf * lax.rsqrt(jnp.mean(xf * xf, axis=-1, keepdims=True) + EPS) * g.astype(jnp.float32)
    return y.astype(x.dtype)


def l2norm(x):
    return x * lax.rsqrt(jnp.sum(x * x, axis=-1, keepdims=True) + EPS)


def rope(t, positions):
    dh = t.shape[-1]
    half = dh // 2
    inv = 1.0 / (ROPE_THETA ** (jnp.arange(half, dtype=jnp.float32) * (2.0 / dh)))
    ang = positions.astype(jnp.float32)[:, None] * inv[None, :]
    cos = jnp.cos(ang)[None, :, None, :]
    sin = jnp.sin(ang)[None, :, None, :]
    tf = t.astype(jnp.float32)
    t1, t2 = tf[..., :half], tf[..., half:]
    out = jnp.concatenate([t1 * cos - t2 * sin, t2 * cos + t1 * sin], axis=-1)
    return out.astype(t.dtype)


def sliding_window_sink_attention(q, k, v, sinks):
    B, S = q.shape[0], q.shape[1]
    nb = S // ATTN_BLOCK
    qb = q.reshape(B, nb, ATTN_BLOCK, ATTN_KV_HEADS, ATTN_GROUPS, ATTN_HEAD_DIM)
    kb = k.reshape(B, nb, ATTN_BLOCK, ATTN_KV_HEADS, ATTN_HEAD_DIM)
    vb = v.reshape(B, nb, ATTN_BLOCK, ATTN_KV_HEADS, ATTN_HEAD_DIM)

    def with_prev(t):
        prev = jnp.concatenate([jnp.zeros_like(t[:, :1]), t[:, :-1]], axis=1)
        return jnp.concatenate([prev, t], axis=2)

    kw, vw = with_prev(kb), with_prev(vb)
    scale = ATTN_HEAD_DIM ** -0.5
    s = jnp.einsum('bnqhgd,bnkhd->bnhgqk', qb, kw).astype(jnp.float32) * scale
    blk = jnp.arange(nb)[:, None, None]
    qpos = blk * ATTN_BLOCK + jnp.arange(ATTN_BLOCK)[None, :, None]
    kpos = blk * ATTN_BLOCK - ATTN_BLOCK + jnp.arange(2 * ATTN_BLOCK)[None, None, :]
    valid = (kpos <= qpos) & (qpos - kpos < WINDOW) & (kpos >= 0)
    s = jnp.where(valid[:, None, None, :, :], s, -jnp.inf)
    sink = jnp.broadcast_to(
        sinks.astype(jnp.float32).reshape(1, 1, ATTN_KV_HEADS, ATTN_GROUPS, 1, 1),
        s.shape[:-1] + (1,))
    probs = jax.nn.softmax(jnp.concatenate([s, sink], axis=-1), axis=-1)[..., :-1]
    o = jnp.einsum('bnhgqk,bnkhd->bnqhgd', probs.astype(v.dtype), vw)
    return o.reshape(B, S, ATTN_Q)


def causal_conv(x, w):
    c = x.shape[-1]
    return lax.conv_general_dilated(
        x, w[:, None, :].astype(x.dtype), window_strides=(1,),
        padding=((DN_CONV - 1, 0),), dimension_numbers=('NWC', 'WIO', 'NWC'),
        feature_group_count=c)


def chunk_gated_delta_rule(q, k, v, g, beta):
    B, S, H, DK = q.shape
    DV = v.shape[-1]
    C = DN_CHUNK
    nc = S // C

    def to_chunks(t):
        t = jnp.moveaxis(t, 2, 1)
        return t.reshape(t.shape[:2] + (nc, C) + t.shape[3:])

    q, k, v, g, beta = (to_chunks(t) for t in (q, k, v, g, beta))
    g = jnp.cumsum(g, axis=-1)
    tril = jnp.tril(jnp.ones((C, C), dtype=bool))
    strict = jnp.tril(jnp.ones((C, C), dtype=bool), -1)
    decay = jnp.exp(jnp.where(tril, g[..., :, None] - g[..., None, :], -jnp.inf))
    k_beta = k * beta[..., None]
    v_beta = v * beta[..., None]
    L = jnp.where(strict, jnp.einsum('bhncd,bhnsd->bhncs', k_beta, k) * decay, 0.0)
    eye = jnp.eye(C, dtype=q.dtype)
    T = lax.linalg.triangular_solve(eye + L, jnp.broadcast_to(eye, L.shape),
                                    left_side=True, lower=True)
    u = jnp.einsum('bhncs,bhnsd->bhncd', T, v_beta)
    w = jnp.einsum('bhncs,bhnsd->bhncd', T, k_beta * jnp.exp(g)[..., None])
    a_qk = jnp.einsum('bhncd,bhnsd->bhncs', q, k) * decay
    q_g = q * jnp.exp(g)[..., None]
    g_last = g[..., -1]
    k_d = k * jnp.exp(g_last[..., None] - g)[..., None]
    d_last = jnp.exp(g_last)

    xs = tuple(jnp.moveaxis(t, 2, 0) for t in (q_g, k_d, u, w, a_qk, d_last))

    def step(state, inp):
        q_i, k_i, u_i, w_i, a_i, d_i = inp
        v_new = u_i - jnp.einsum('bhck,bhkv->bhcv', w_i, state)
        o = jnp.einsum('bhck,bhkv->bhcv', q_i, state) + jnp.einsum('bhcs,bhsv->bhcv', a_i, v_new)
        state = state * d_i[..., None, None] + jnp.einsum('bhck,bhcv->bhkv', k_i, v_new)
        return state, o

    s0 = jnp.zeros((B, H, DK, DV), dtype=q.dtype)
    _, o = lax.scan(step, s0, xs)
    o = jnp.moveaxis(o, 0, 2).reshape(B, H, S, DV)
    return jnp.moveaxis(o, 1, 2)


def gated_deltanet(q, k, v, z, b, a, conv_w, a_log, dt_bias, norm_w):
    B, S = q.shape[0], q.shape[1]
    qkv = jax.nn.silu(causal_conv(jnp.concatenate([q, k, v], axis=-1), conv_w))
    qc, kc, vc = jnp.split(qkv.astype(jnp.float32), 3, axis=-1)
    shp = (B, S, DN_HEADS, DN_HEAD_DIM)
    qc = l2norm(qc.reshape(shp)) * (DN_HEAD_DIM ** -0.5)
    kc = l2norm(kc.reshape(shp))
    vc = vc.reshape(shp)
    beta = jax.nn.sigmoid(b.astype(jnp.float32))
    g = -jnp.exp(a_log.astype(jnp.float32)) * jax.nn.softplus(
        a.astype(jnp.float32) + dt_bias.astype(jnp.float32))
    o = chunk_gated_delta_rule(qc, kc, vc, g, beta)
    o = o * lax.rsqrt(jnp.mean(o * o, axis=-1, keepdims=True) + EPS) * norm_w.astype(jnp.float32)
    o = o * jax.nn.silu(z.astype(jnp.float32).reshape(shp))
    return o.reshape(B, S, DN_W).astype(q.dtype)


def setup_inputs(seed: int = 0) -> dict:
    key = jax.random.key(seed)
    ks = jax.random.split(key, 20)
    f32 = jnp.float32
    nrm = lambda k, shape, s: jax.random.normal(k, shape, f32) * s
    gain = lambda k, shape: 1.0 + 0.02 * jax.random.normal(k, shape, f32)
    dt = jnp.exp(jax.random.uniform(ks[6], (DEPTH, DN_HEADS), f32, np.log(1e-3), np.log(1e-1)))
    return {
        "x": nrm(ks[0], (BATCH, SEQ, D_MODEL), 1.0),
        "p": nrm(ks[1], (DEPTH, BATCH, SEQ, PLE_DIM), 1.0),
        "norm_mix": gain(ks[2], (DEPTH, D_MODEL)),
        "w_in": nrm(ks[3], (DEPTH, D_MODEL, D_IN), D_MODEL ** -0.5),
        "conv_w": nrm(ks[4], (DEPTH, DN_CONV, CONV_CH), DN_CONV ** -0.5),
        "a_log": jnp.log(jax.random.uniform(ks[5], (DEPTH, DN_HEADS), f32, 1.0, 16.0)),
        "dt_bias": dt + jnp.log(-jnp.expm1(-dt)),
        "dn_norm": gain(ks[7], (DEPTH, DN_HEAD_DIM)),
        "sinks": nrm(ks[8], (DEPTH, ATTN_HEADS), 0.5),
        "w_o": nrm(ks[9], (DEPTH, MIX_WIDTH, D_MODEL), MIX_WIDTH ** -0.5),
        "norm_mlp": gain(ks[10], (DEPTH, D_MODEL)),
        "w_up": nrm(ks[11], (DEPTH, D_MODEL, D_FF), D_MODEL ** -0.5),
        "w_down": nrm(ks[12], (DEPTH, D_FF, D_MODEL), D_FF ** -0.5),
        "norm_ple": gain(ks[13], (DEPTH, D_MODEL)),
        "w_ple_gate": nrm(ks[14], (DEPTH, D_MODEL, D_MODEL), D_MODEL ** -0.5),
        "w_ple_proj": nrm(ks[15], (DEPTH, PLE_DIM, D_MODEL), PLE_DIM ** -0.5),
        "norm_final": gain(ks[16], (D_MODEL,)),
    }


def reference(x, p, norm_mix, w_in, conv_w, a_log, dt_bias, dn_norm, sinks, w_o,
              norm_mlp, w_up, w_down, norm_ple, w_ple_gate, w_ple_proj, norm_final):
    B, S, _ = x.shape
    positions = jnp.arange(S)
    split_idx = np.cumsum(SPLIT_SIZES)[:-1].tolist()
    h = x
    for i in range(DEPTH):
        u = rmsnorm(h, norm_mix[i])
        proj = u @ w_in[i]
        aq, ak, av, dq, dk, dv, dz, db, da = jnp.split(proj, split_idx, axis=-1)
        aq = rope(aq.reshape(B, S, ATTN_HEADS, ATTN_HEAD_DIM), positions)
        ak = rope(ak.reshape(B, S, ATTN_KV_HEADS, ATTN_HEAD_DIM), positions)
        av = av.reshape(B, S, ATTN_KV_HEADS, ATTN_HEAD_DIM)
        attn_out = sliding_window_sink_attention(aq, ak, av, sinks[i])
        dn_out = gated_deltanet(dq, dk, dv, dz, db, da, conv_w[i], a_log[i],
                                dt_bias[i], dn_norm[i])
        h = h + jnp.concatenate([attn_out, dn_out], axis=-1) @ w_o[i]
        m = rmsnorm(h, norm_mlp[i])
        h = h + jnp.square(jax.nn.relu(m @ w_up[i])) @ w_down[i]
        gate = jax.nn.sigmoid(rmsnorm(h, norm_ple[i]) @ w_ple_gate[i])
        h = h + gate * (p[i] @ w_ple_proj[i])
    return rmsnorm(h, norm_final)
```

```python
import functools

import jax
import jax.numpy as jnp
import numpy as np
from jax import lax
from jax.experimental import pallas as pl
from jax.experimental.pallas import tpu as pltpu

F32 = jnp.float32
BF16 = jnp.bfloat16

D_MODEL = 1024
PLE_DIM = 256
ATTN_HEADS = 8
ATTN_KV_HEADS = 2
ATTN_HEAD_DIM = 64
ATTN_BLOCK = 128
ROPE_THETA = 10000.0
DN_HEADS = 4
DN_HEAD_DIM = 128
DN_CONV = 4
DN_CHUNK = 64
D_FF = 4 * D_MODEL
EPS = 1e-6
ATTN_Q = ATTN_HEADS * ATTN_HEAD_DIM
ATTN_KV = ATTN_KV_HEADS * ATTN_HEAD_DIM
DN_W = DN_HEADS * DN_HEAD_DIM

LANES = 128
PAIR = 2 * DN_CHUNK
NEG = -1e30

PROJ_ROWS = 512
DN_ROWS = 256
TAIL_ROWS = 512
FF_CHUNK = 1024
VMEM_LIMIT = 56 * 1024 * 1024


def _dot(a, b):
    return jnp.dot(a, b, preferred_element_type=F32)


def _dot_nt(a, b):
    return lax.dot_general(a, b, (((1,), (1,)), ((), ())), preferred_element_type=F32)


def _dot_tn(a, b):
    return lax.dot_general(a, b, (((0,), (0,)), ((), ())), preferred_element_type=F32)


def _rms(x, g):
    return x * lax.rsqrt(jnp.mean(x * x, axis=-1, keepdims=True) + EPS) * g


def _rope(t, cos, sin_signed, first_half):
    width = t.shape[-1]
    reps = width // LANES
    fwd = pltpu.roll(t, ATTN_HEAD_DIM // 2, axis=1)
    bwd = pltpu.roll(t, width - ATTN_HEAD_DIM // 2, axis=1)
    partner = jnp.where(jnp.tile(first_half, (1, reps)), bwd, fwd)
    return t * jnp.tile(cos, (1, reps)) + partner * jnp.tile(sin_signed, (1, reps))


def _proj_kernel(x_ref, g_ref, wq_ref, wk_ref, wv_ref, wd_ref, wba_ref, cos_ref, sin_ref,
                 q_out, k_out, v_out, d_out, ba_out):
    u = _rms(x_ref[...], g_ref[...]).astype(BF16)
    cos = cos_ref[...]
    sin_signed = sin_ref[...]
    lane = lax.broadcasted_iota(jnp.int32, cos.shape, 1)
    first_half = (lane % ATTN_HEAD_DIM) < (ATTN_HEAD_DIM // 2)

    q = _rope(_dot(u, wq_ref[...]), cos, sin_signed, first_half)
    q_out[...] = (q * (ATTN_HEAD_DIM ** -0.5)).astype(BF16)
    k = _rope(_dot(u, wk_ref[...]), cos, sin_signed, first_half)
    k_out[...] = k.astype(BF16)
    v_out[...] = _dot(u, wv_ref[...]).astype(BF16)
    for j in range(4):
        sl = slice(j * DN_W, (j + 1) * DN_W)
        d_out[:, sl] = _dot(u, wd_ref[:, sl]).astype(BF16)
    ba_out[...] = _dot_nt(wba_ref[...], u)


def _attn_kernel(sink_ref, q_ref, kc_ref, kp_ref, vc_ref, vp_ref, o_ref):
    n = pl.program_id(1)
    row = lax.broadcasted_iota(jnp.int32, (ATTN_BLOCK, 2 * ATTN_BLOCK), 0)
    col = lax.broadcasted_iota(jnp.int32, (ATTN_BLOCK, 2 * ATTN_BLOCK), 1)
    valid = ((col < ATTN_BLOCK) & (col > row) & (n > 0)) | ((col >= ATTN_BLOCK) & ((col - ATTN_BLOCK) <= row))
    lane = lax.broadcasted_iota(jnp.int32, (ATTN_BLOCK, LANES), 1)
    low = lane < ATTN_HEAD_DIM

    for pair in range(ATTN_HEADS // 2):
        grp = (2 * pair) // (ATTN_HEADS // ATTN_KV_HEADS)
        gs = slice(grp * LANES, (grp + 1) * LANES)
        kw = jnp.concatenate([kp_ref[:, gs], kc_ref[:, gs]], axis=0)
        vw = jnp.concatenate([vp_ref[:, gs], vc_ref[:, gs]], axis=0)
        q2 = q_ref[:, pair * LANES:(pair + 1) * LANES]
        outs = []
        for half in range(2):
            head = 2 * pair + half
            keep = low if half == 0 else jnp.logical_not(low)
            qm = jnp.where(keep, q2, jnp.zeros_like(q2))
            s = jnp.where(valid, _dot_nt(qm, kw), NEG)
            sink = sink_ref[head]
            m = jnp.maximum(jnp.max(s, axis=1, keepdims=True), sink)
            p = jnp.exp(s - m)
            denom = jnp.sum(p, axis=1, keepdims=True) + jnp.exp(sink - m)
            outs.append(_dot(p.astype(BF16), vw) / denom)
        o_ref[:, pair * LANES:(pair + 1) * LANES] = jnp.where(low, outs[0], outs[1]).astype(BF16)


def _silu(x):
    return x / (1.0 + jnp.exp(-x))


def _dn_kernel(dq_ref, dk_ref, dv_ref, dz_ref, ba_ref, cw_ref, alog_ref, dtb_ref, nw_ref,
               o_ref, xbuf, state):
    t = pl.program_id(1)
    rows = dq_ref.shape[0]
    halo = 8

    @pl.when(t == 0)
    def _():
        xbuf[0:halo, :] = jnp.zeros((halo, 3 * DN_W), F32)
        state[...] = jnp.zeros(state.shape, F32)

    xbuf[halo:halo + rows, 0:DN_W] = dq_ref[...].astype(F32)
    xbuf[halo:halo + rows, DN_W:2 * DN_W] = dk_ref[...].astype(F32)
    xbuf[halo:halo + rows, 2 * DN_W:3 * DN_W] = dv_ref[...].astype(F32)

    def conv_silu(c0):
        cs = slice(c0, c0 + DN_HEAD_DIM)
        acc = xbuf[pl.ds(halo, rows), cs] * cw_ref[DN_CONV - 1:DN_CONV, cs]
        for j in range(DN_CONV - 1):
            acc = acc + xbuf[pl.ds(halo - (DN_CONV - 1) + j, rows), cs] * cw_ref[j:j + 1, cs]
        return _silu(acc)

    beta_all = jax.nn.sigmoid(ba_ref[0:8, :])
    a_all = ba_ref[8:16, :] + dtb_ref[...]
    softplus = jnp.maximum(a_all, 0.0) + jnp.log(1.0 + jnp.exp(-jnp.abs(a_all)))
    g_all = -jnp.exp(alog_ref[...]) * softplus

    ri = lax.broadcasted_iota(jnp.int32, (PAIR, PAIR), 0)
    ci = lax.broadcasted_iota(jnp.int32, (PAIR, PAIR), 1)
    same = (ri // DN_CHUNK) == (ci // DN_CHUNK)
    tril = same & (ci <= ri)
    strict = same & (ci < ri)
    eye = ci == ri
    eye_f = eye.astype(F32)

    for h in range(DN_HEADS):
        hs = slice(h * DN_HEAD_DIM, (h + 1) * DN_HEAD_DIM)
        q = conv_silu(h * DN_HEAD_DIM)
        k = conv_silu(DN_W + h * DN_HEAD_DIM)
        v = conv_silu(2 * DN_W + h * DN_HEAD_DIM)
        q = q * lax.rsqrt(jnp.sum(q * q, axis=-1, keepdims=True) + EPS) * (DN_HEAD_DIM ** -0.5)
        k = k * lax.rsqrt(jnp.sum(k * k, axis=-1, keepdims=True) + EPS)
        s_h = state[h]
        o_parts = []
        for pr in range(rows // PAIR):
            rs = slice(pr * PAIR, (pr + 1) * PAIR)
            qp, kp, vp = q[rs], k[rs], v[rs]
            g_row = g_all[h:h + 1, rs]
            beta_row = beta_all[h:h + 1, rs]
            gc_col = jnp.sum(jnp.where(tril, g_row, 0.0), axis=1, keepdims=True)
            glast_col = jnp.sum(jnp.where(same, g_row, 0.0), axis=1, keepdims=True)
            beta_col = jnp.sum(jnp.where(eye, beta_row, 0.0), axis=1, keepdims=True)
            gc_rowb = jnp.transpose(jnp.broadcast_to(gc_col, (PAIR, PAIR)))
            decay = jnp.exp(jnp.where(tril, gc_col - gc_rowb, NEG))
            kb = kp.astype(BF16)
            kk = _dot_nt(kb, kb)
            qk = _dot_nt(qp.astype(BF16), kb)
            x = jnp.where(strict, -(kk * beta_col * decay), 0.0)
            tm = eye_f + x
            y = x
            for lvl in range(5):
                yb = y.astype(BF16)
                y = _dot(yb, yb)
                tm = tm + _dot(tm.astype(BF16), y.astype(BF16))
            eg_row = jnp.exp(gc_rowb[0:1, :])
            u = _dot((tm * beta_row).astype(BF16), vp.astype(BF16))
            w = _dot((tm * (beta_row * eg_row)).astype(BF16), kb)
            a = jnp.where(tril, qk * decay, 0.0).astype(BF16)
            qg = (qp * jnp.exp(gc_col)).astype(BF16)
            kd = (kp * jnp.exp(glast_col - gc_col)).astype(BF16)
            vnews = []
            for c in range(2):
                cr = slice(c * DN_CHUNK, (c + 1) * DN_CHUNK)
                sb = s_h.astype(BF16)
                vnew = u[cr] - _dot(w[cr].astype(BF16), sb)
                vnews.append(vnew)
                if c == 0:
                    vfull = jnp.concatenate([vnew, jnp.zeros_like(vnew)], axis=0)
                else:
                    vfull = jnp.concatenate(vnews, axis=0)
                o_parts.append(_dot(qg[cr], sb) + _dot(a[cr], vfull.astype(BF16)))
                d_last = jnp.exp(glast_col[c * DN_CHUNK:c * DN_CHUNK + 1, :])
                s_h = s_h * d_last + _dot_tn(kd[cr], vnew.astype(BF16))
        state[h] = s_h
        o = jnp.concatenate(o_parts, axis=0)
        o = o * lax.rsqrt(jnp.mean(o * o, axis=-1, keepdims=True) + EPS) * nw_ref[...]
        o_ref[:, hs] = (o * _silu(dz_ref[:, hs].astype(F32))).astype(BF16)

    xbuf[0:halo, :] = xbuf[rows:rows + halo, :]


def _tail_kernel(x_ref, at_ref, dn_ref, p_ref, wo_ref, gm_ref, wu_ref, wd_ref, gp_ref, wg_ref, wp_ref,
                 gf_ref, o_ref):
    h = x_ref[...] + _dot(at_ref[...], wo_ref[0:ATTN_Q, :]) + _dot(dn_ref[...], wo_ref[ATTN_Q:, :])
    m = _rms(h, gm_ref[...]).astype(BF16)
    acc = jnp.zeros(h.shape, F32)
    for c in range(D_FF // FF_CHUNK):
        up = jnp.maximum(_dot(m, wu_ref[c]), 0.0)
        acc = acc + _dot((up * up).astype(BF16), wd_ref[c])
    h = h + acc
    gate = jax.nn.sigmoid(_dot(_rms(h, gp_ref[...]).astype(BF16), wg_ref[...]))
    h = h + gate * _dot(p_ref[...].astype(BF16), wp_ref[...])
    o_ref[...] = _rms(h, gf_ref[...])


def _const_spec(shape):
    nd = len(shape)
    return pl.BlockSpec(shape, lambda *_: (0,) * nd, pipeline_mode=pl.Buffered(1))


def _rope_tables(seq):
    half = ATTN_HEAD_DIM // 2
    inv = 1.0 / (ROPE_THETA ** (jnp.arange(half, dtype=F32) * (2.0 / ATTN_HEAD_DIM)))
    ang = jnp.arange(seq, dtype=F32)[:, None] * inv[None, :]
    cos, sin = jnp.cos(ang), jnp.sin(ang)
    cos_t = jnp.tile(cos, (1, LANES // half))
    sin_t = jnp.tile(jnp.concatenate([-sin, sin], axis=1), (1, LANES // ATTN_HEAD_DIM))
    return cos_t, sin_t


def kernel(x, p, norm_mix, w_in, conv_w, a_log, dt_bias, dn_norm, sinks, w_o, norm_mlp, w_up, w_down,
           norm_ple, w_ple_gate, w_ple_proj, norm_final):
    B, S, D = x.shape
    N = B * S
    assert D == D_MODEL and p.shape[0] == 1 and S % PROJ_ROWS == 0 and S % DN_ROWS == 0 and N % TAIL_ROWS == 0
    x2 = x.reshape(N, D)
    p2 = p[0].reshape(N, PLE_DIM)
    params = pltpu.CompilerParams

    w = w_in[0]
    c = np.cumsum((ATTN_Q, ATTN_KV, ATTN_KV, DN_W, DN_W, DN_W, DN_W, DN_HEADS, DN_HEADS)).tolist()
    wq = w[:, :c[0]].astype(BF16)

    def dup(cols):
        h0, h1 = cols[:, :ATTN_HEAD_DIM], cols[:, ATTN_HEAD_DIM:]
        return jnp.concatenate([h0, h0, h1, h1], axis=1).astype(BF16)

    wk = dup(w[:, c[0]:c[1]])
    wv = dup(w[:, c[1]:c[2]])
    wd = w[:, c[2]:c[6]].astype(BF16)
    zpad = jnp.zeros((8 - DN_HEADS, D), F32)
    wba = jnp.concatenate([w[:, c[6]:c[7]].T, zpad, w[:, c[7]:c[8]].T, zpad], axis=0).astype(BF16)
    cos_t, sin_t = _rope_tables(S)

    sp = S // PROJ_ROWS
    row_blk = lambda width: pl.BlockSpec((PROJ_ROWS, width), lambda b, s: (b * sp + s, 0))
    q_r, k_r, v_r, d_r, ba_t = pl.pallas_call(
        _proj_kernel,
        grid=(B, sp),
        in_specs=[row_blk(D), _const_spec((1, D)), _const_spec(wq.shape), _const_spec(wk.shape),
                  _const_spec(wv.shape), _const_spec(wd.shape), _const_spec(wba.shape),
                  pl.BlockSpec((PROJ_ROWS, LANES), lambda b, s: (s, 0)),
                  pl.BlockSpec((PROJ_ROWS, LANES), lambda b, s: (s, 0))],
        out_specs=[row_blk(ATTN_Q), row_blk(2 * ATTN_KV), row_blk(2 * ATTN_KV), row_blk(4 * DN_W),
                   pl.BlockSpec((16, PROJ_ROWS), lambda b, s: (0, b * sp + s))],
        out_shape=[jax.ShapeDtypeStruct((N, ATTN_Q), BF16), jax.ShapeDtypeStruct((N, 2 * ATTN_KV), BF16),
                   jax.ShapeDtypeStruct((N, 2 * ATTN_KV), BF16), jax.ShapeDtypeStruct((N, 4 * DN_W), BF16),
                   jax.ShapeDtypeStruct((16, N), F32)],
        compiler_params=params(dimension_semantics=("parallel", "parallel"), vmem_limit_bytes=VMEM_LIMIT),
        name="proj",
    )(x2, norm_mix[0].reshape(1, D), wq, wk, wv, wd, wba, cos_t, sin_t)

    nb = S // ATTN_BLOCK
    cur = lambda width: pl.BlockSpec((ATTN_BLOCK, width), lambda b, n: (b * nb + n, 0))
    prev = lambda width: pl.BlockSpec((ATTN_BLOCK, width), lambda b, n: (b * nb + jnp.maximum(n - 1, 0), 0))
    attn = pl.pallas_call(
        _attn_kernel,
        grid=(B, nb),
        in_specs=[pl.BlockSpec(memory_space=pltpu.SMEM), cur(ATTN_Q), cur(2 * ATTN_KV), prev(2 * ATTN_KV),
                  cur(2 * ATTN_KV), prev(2 * ATTN_KV)],
        out_specs=cur(ATTN_Q),
        out_shape=jax.ShapeDtypeStruct((N, ATTN_Q), BF16),
        compiler_params=params(dimension_semantics=("parallel", "parallel"), vmem_limit_bytes=VMEM_LIMIT),
        name="attn",
    )(sinks[0].astype(F32), q_r, k_r, k_r, v_r, v_r)

    nt = S // DN_ROWS
    dcol = lambda j: pl.BlockSpec((DN_ROWS, DN_W), lambda b, t: (b * nt + t, j))
    pad8 = lambda v: jnp.concatenate([v.astype(F32), jnp.zeros((8 - DN_HEADS,), F32)]).reshape(8, 1)
    dn = pl.pallas_call(
        _dn_kernel,
        grid=(B, nt),
        in_specs=[dcol(0), dcol(1), dcol(2), dcol(3),
                  pl.BlockSpec((16, DN_ROWS), lambda b, t: (0, b * nt + t)),
                  _const_spec((DN_CONV, 3 * DN_W)), _const_spec((8, 1)), _const_spec((8, 1)),
                  _const_spec((1, DN_HEAD_DIM))],
        out_specs=pl.BlockSpec((DN_ROWS, DN_W), lambda b, t: (b * nt + t, 0)),
        out_shape=jax.ShapeDtypeStruct((N, DN_W), BF16),
        scratch_shapes=[pltpu.VMEM((DN_ROWS + 8, 3 * DN_W), F32),
                        pltpu.VMEM((DN_HEADS, DN_HEAD_DIM, DN_HEAD_DIM), F32)],
        compiler_params=params(dimension_semantics=("parallel", "arbitrary"), vmem_limit_bytes=VMEM_LIMIT),
        name="deltanet",
    )(d_r, d_r, d_r, d_r, ba_t, conv_w[0].astype(F32), pad8(a_log[0]), pad8(dt_bias[0]),
      dn_norm[0].reshape(1, DN_HEAD_DIM).astype(F32))

    nc = D_FF // FF_CHUNK
    wu = w_up[0].reshape(D, nc, FF_CHUNK).transpose(1, 0, 2).astype(BF16)
    wdn = w_down[0].reshape(nc, FF_CHUNK, D).astype(BF16)
    tail_blk = lambda width: pl.BlockSpec((TAIL_ROWS, width), lambda i: (i, 0))
    vec = lambda g: g.reshape(1, D).astype(F32)
    out = pl.pallas_call(
        _tail_kernel,
        grid=(N // TAIL_ROWS,),
        in_specs=[tail_blk(D), tail_blk(ATTN_Q), tail_blk(DN_W), tail_blk(PLE_DIM),
                  _const_spec((D, D)), _const_spec((1, D)), _const_spec(wu.shape), _const_spec(wdn.shape),
                  _const_spec((1, D)), _const_spec((D, D)), _const_spec((PLE_DIM, D)), _const_spec((1, D))],
        out_specs=tail_blk(D),
        out_shape=jax.ShapeDtypeStruct((N, D), F32),
        compiler_params=params(dimension_semantics=("parallel",), vmem_limit_bytes=VMEM_LIMIT),
        name="tail",
    )(x2, attn, dn, p2, w_o[0].astype(BF16), vec(norm_mlp[0]), wu, wdn, vec(norm_ple[0]),
      w_ple_gate[0].astype(BF16), w_ple_proj[0].astype(BF16), vec(norm_final))
    return out.reshape(B, S, D)
```

```python
import functools

import jax
import jax.numpy as jnp
import numpy as np
from jax import lax
from jax.experimental import pallas as pl
from jax.experimental.pallas import tpu as pltpu

F32 = jnp.float32
BF16 = jnp.bfloat16

D_MODEL = 1024
PLE_DIM = 256
ATTN_HEADS = 8
ATTN_KV_HEADS = 2
ATTN_HEAD_DIM = 64
ATTN_BLOCK = 128
ROPE_THETA = 10000.0
DN_HEADS = 4
DN_HEAD_DIM = 128
DN_CONV = 4
DN_CHUNK = 64
D_FF = 4 * D_MODEL
EPS = 1e-6
ATTN_Q = ATTN_HEADS * ATTN_HEAD_DIM
ATTN_KV = ATTN_KV_HEADS * ATTN_HEAD_DIM
DN_W = DN_HEADS * DN_HEAD_DIM

LANES = 128
PAIR = 2 * DN_CHUNK
NEG = -1e30

PROJ_ROWS = 512
DN_ROWS = 256
TAIL_ROWS = 512
FF_CHUNK = 1024
VMEM_LIMIT = 56 * 1024 * 1024


def _dot(a, b):
    return jnp.dot(a, b, preferred_element_type=F32)


def _dot_nt(a, b):
    return lax.dot_general(a, b, (((1,), (1,)), ((), ())), preferred_element_type=F32)


def _dot_tn(a, b):
    return lax.dot_general(a, b, (((0,), (0,)), ((), ())), preferred_element_type=F32)


def _rms(x, g):
    return x * lax.rsqrt(jnp.mean(x * x, axis=-1, keepdims=True) + EPS) * g


def _rope(t, cos, sin_signed, first_half):
    width = t.shape[-1]
    reps = width // LANES
    fwd = pltpu.roll(t, ATTN_HEAD_DIM // 2, axis=1)
    bwd = pltpu.roll(t, width - ATTN_HEAD_DIM // 2, axis=1)
    partner = jnp.where(jnp.tile(first_half, (1, reps)), bwd, fwd)
    return t * jnp.tile(cos, (1, reps)) + partner * jnp.tile(sin_signed, (1, reps))


def _proj_kernel(x_ref, g_ref, wq_ref, wk_ref, wv_ref, wd_ref, wba_ref, cos_ref, sin_ref,
                 q_out, k_out, v_out, d_out, ba_out):
    u = _rms(x_ref[...], g_ref[...]).astype(BF16)
    cos = cos_ref[...]
    sin_signed = sin_ref[...]
    lane = lax.broadcasted_iota(jnp.int32, cos.shape, 1)
    first_half = (lane % ATTN_HEAD_DIM) < (ATTN_HEAD_DIM // 2)

    q = _rope(_dot(u, wq_ref[...]), cos, sin_signed, first_half)
    q_out[...] = (q * (ATTN_HEAD_DIM ** -0.5)).astype(BF16)
    k = _rope(_dot(u, wk_ref[...]), cos, sin_signed, first_half)
    k_out[...] = k.astype(BF16)
    v_out[...] = _dot(u, wv_ref[...]).astype(BF16)
    for j in range(4):
        sl = slice(j * DN_W, (j + 1) * DN_W)
        d_out[:, sl] = _dot(u, wd_ref[:, sl]).astype(BF16)
    ba_out[...] = _dot_nt(wba_ref[...], u)


def _attn_kernel(sink_ref, q_ref, kc_ref, kp_ref, vc_ref, vp_ref, o_ref):
    n = pl.program_id(1)
    row = lax.broadcasted_iota(jnp.int32, (ATTN_BLOCK, 2 * ATTN_BLOCK), 0)
    col = lax.broadcasted_iota(jnp.int32, (ATTN_BLOCK, 2 * ATTN_BLOCK), 1)
    valid = ((col < ATTN_BLOCK) & (col > row) & (n > 0)) | ((col >= ATTN_BLOCK) & ((col - ATTN_BLOCK) <= row))
    lane = lax.broadcasted_iota(jnp.int32, (ATTN_BLOCK, LANES), 1)
    low = lane < ATTN_HEAD_DIM

    for pair in range(ATTN_HEADS // 2):
        grp = (2 * pair) // (ATTN_HEADS // ATTN_KV_HEADS)
        gs = slice(grp * LANES, (grp + 1) * LANES)
        kw = jnp.concatenate([kp_ref[:, gs], kc_ref[:, gs]], axis=0)
        vw = jnp.concatenate([vp_ref[:, gs], vc_ref[:, gs]], axis=0)
        q2 = q_ref[:, pair * LANES:(pair + 1) * LANES]
        outs = []
        for half in range(2):
            head = 2 * pair + half
            keep = low if half == 0 else jnp.logical_not(low)
            qm = jnp.where(keep, q2, jnp.zeros_like(q2))
            s = jnp.where(valid, _dot_nt(qm, kw), NEG)
            sink = sink_ref[head]
            m = jnp.maximum(jnp.max(s, axis=1, keepdims=True), sink)
            p = jnp.exp(s - m)
            denom = jnp.sum(p, axis=1, keepdims=True) + jnp.exp(sink - m)
            outs.append(_dot(p.astype(BF16), vw) / denom)
        o_ref[:, pair * LANES:(pair + 1) * LANES] = jnp.where(low, outs[0], outs[1]).astype(BF16)


def _silu(x):
    return x / (1.0 + jnp.exp(-x))


def _dn_kernel(dq_ref, dk_ref, dv_ref, dz_ref, ba_ref, cw_ref, alog_ref, dtb_ref, nw_ref,
               o_ref, xbuf, state):
    t = pl.program_id(1)
    rows = dq_ref.shape[0]
    halo = 8

    @pl.when(t == 0)
    def _():
        xbuf[0:halo, :] = jnp.zeros((halo, 3 * DN_W), F32)
        state[...] = jnp.zeros(state.shape, F32)

    xbuf[halo:halo + rows, 0:DN_W] = dq_ref[...].astype(F32)
    xbuf[halo:halo + rows, DN_W:2 * DN_W] = dk_ref[...].astype(F32)
    xbuf[halo:halo + rows, 2 * DN_W:3 * DN_W] = dv_ref[...].astype(F32)

    def conv_silu(c0):
        cs = slice(c0, c0 + DN_HEAD_DIM)
        acc = xbuf[pl.ds(halo, rows), cs] * cw_ref[DN_CONV - 1:DN_CONV, cs]
        for j in range(DN_CONV - 1):
            acc = acc + xbuf[pl.ds(halo - (DN_CONV - 1) + j, rows), cs] * cw_ref[j:j + 1, cs]
        return _silu(acc)

    beta_all = jax.nn.sigmoid(ba_ref[0:8, :])
    a_all = ba_ref[8:16, :] + dtb_ref[...]
    softplus = jnp.maximum(a_all, 0.0) + jnp.log(1.0 + jnp.exp(-jnp.abs(a_all)))
    g_all = -jnp.exp(alog_ref[...]) * softplus

    ri = lax.broadcasted_iota(jnp.int32, (PAIR, PAIR), 0)
    ci = lax.broadcasted_iota(jnp.int32, (PAIR, PAIR), 1)
    same = (ri // DN_CHUNK) == (ci // DN_CHUNK)
    tril = same & (ci <= ri)
    strict = same & (ci < ri)
    eye = ci == ri
    eye_f = eye.astype(F32)

    heads = range(DN_HEADS)
    pairs = range(rows // PAIR)
    items = [(h, pr) for pr in pairs for h in heads]
    q, k, v = [], [], []
    for h in heads:
        qh = conv_silu(h * DN_HEAD_DIM)
        kh = conv_silu(DN_W + h * DN_HEAD_DIM)
        v.append(conv_silu(2 * DN_W + h * DN_HEAD_DIM))
        q.append(qh * lax.rsqrt(jnp.sum(qh * qh, axis=-1, keepdims=True) + EPS) * (DN_HEAD_DIM ** -0.5))
        k.append(kh * lax.rsqrt(jnp.sum(kh * kh, axis=-1, keepdims=True) + EPS))

    it = {}
    for h, pr in items:
        rs = slice(pr * PAIR, (pr + 1) * PAIR)
        d = dict(q=q[h][rs], k=k[h][rs], v=v[h][rs])
        g_row = g_all[h:h + 1, rs]
        d["beta_row"] = beta_all[h:h + 1, rs]
        d["gc_col"] = jnp.sum(jnp.where(tril, g_row, 0.0), axis=1, keepdims=True)
        d["glast_col"] = jnp.sum(jnp.where(same, g_row, 0.0), axis=1, keepdims=True)
        beta_col = jnp.sum(jnp.where(eye, d["beta_row"], 0.0), axis=1, keepdims=True)
        gc_rowb = jnp.transpose(jnp.broadcast_to(d["gc_col"], (PAIR, PAIR)))
        d["eg_row"] = jnp.exp(gc_rowb[0:1, :])
        d["decay"] = jnp.exp(jnp.where(tril, d["gc_col"] - gc_rowb, NEG))
        d["kb"] = d["k"].astype(BF16)
        kk = _dot_nt(d["kb"], d["kb"])
        d["qk"] = _dot_nt(d["q"].astype(BF16), d["kb"])
        d["y"] = jnp.where(strict, -(kk * beta_col * d["decay"]), 0.0)
        d["tm"] = eye_f + d["y"]
        it[(h, pr)] = d
    for lvl in range(5):
        for d in it.values():
            yb = d["y"].astype(BF16)
            d["y"] = _dot(yb, yb)
        for d in it.values():
            d["tm"] = d["tm"] + _dot(d["tm"].astype(BF16), d["y"].astype(BF16))
    for d in it.values():
        tm, beta_row = d["tm"], d["beta_row"]
        d["u"] = _dot((tm * beta_row).astype(BF16), d["v"].astype(BF16))
        d["w"] = _dot((tm * (beta_row * d["eg_row"])).astype(BF16), d["kb"])
        d["a"] = jnp.where(tril, d["qk"] * d["decay"], 0.0).astype(BF16)
        d["qg"] = (d["q"] * jnp.exp(d["gc_col"])).astype(BF16)
        d["kd"] = (d["k"] * jnp.exp(d["glast_col"] - d["gc_col"])).astype(BF16)

    s = [state[h] for h in heads]
    o_parts = [[] for _ in heads]
    for pr in pairs:
        vnew0 = [None] * DN_HEADS
        for c in range(2):
            cr = slice(c * DN_CHUNK, (c + 1) * DN_CHUNK)
            sb = [s[h].astype(BF16) for h in heads]
            vnew = [it[(h, pr)]["u"][cr] - _dot(it[(h, pr)]["w"][cr].astype(BF16), sb[h]) for h in heads]
            for h in heads:
                d = it[(h, pr)]
                if c == 0:
                    vnew0[h] = vnew[h]
                    vfull = jnp.concatenate([vnew[h], jnp.zeros_like(vnew[h])], axis=0)
                else:
                    vfull = jnp.concatenate([vnew0[h], vnew[h]], axis=0)
                o_parts[h].append(_dot(d["qg"][cr], sb[h]) + _dot(d["a"][cr], vfull.astype(BF16)))
            for h in heads:
                d = it[(h, pr)]
                d_last = jnp.exp(d["glast_col"][c * DN_CHUNK:c * DN_CHUNK + 1, :])
                s[h] = s[h] * d_last + _dot_tn(d["kd"][cr], vnew[h].astype(BF16))
    for h in heads:
        hs = slice(h * DN_HEAD_DIM, (h + 1) * DN_HEAD_DIM)
        state[h] = s[h]
        o = jnp.concatenate(o_parts[h], axis=0)
        o = o * lax.rsqrt(jnp.mean(o * o, axis=-1, keepdims=True) + EPS) * nw_ref[...]
        o_ref[:, hs] = (o * _silu(dz_ref[:, hs].astype(F32))).astype(BF16)

    xbuf[0:halo, :] = xbuf[rows:rows + halo, :]


def _tail_kernel(x_ref, at_ref, dn_ref, p_ref, wo_ref, gm_ref, wu_ref, wd_ref, gp_ref, wg_ref, wp_ref,
                 gf_ref, o_ref):
    h = x_ref[...] + _dot(at_ref[...], wo_ref[0:ATTN_Q, :]) + _dot(dn_ref[...], wo_ref[ATTN_Q:, :])
    m = _rms(h, gm_ref[...]).astype(BF16)
    acc = jnp.zeros(h.shape, F32)
    for c in range(D_FF // FF_CHUNK):
        up = jnp.maximum(_dot(m, wu_ref[c]), 0.0)
        acc = acc + _dot((up * up).astype(BF16), wd_ref[c])
    h = h + acc
    gate = jax.nn.sigmoid(_dot(_rms(h, gp_ref[...]).astype(BF16), wg_ref[...]))
    h = h + gate * _dot(p_ref[...].astype(BF16), wp_ref[...])
    o_ref[...] = _rms(h, gf_ref[...])


def _const_spec(shape):
    nd = len(shape)
    return pl.BlockSpec(shape, lambda *_: (0,) * nd, pipeline_mode=pl.Buffered(1))


def _rope_tables(seq):
    half = ATTN_HEAD_DIM // 2
    inv = 1.0 / (ROPE_THETA ** (jnp.arange(half, dtype=F32) * (2.0 / ATTN_HEAD_DIM)))
    ang = jnp.arange(seq, dtype=F32)[:, None] * inv[None, :]
    cos, sin = jnp.cos(ang), jnp.sin(ang)
    cos_t = jnp.tile(cos, (1, LANES // half))
    sin_t = jnp.tile(jnp.concatenate([-sin, sin], axis=1), (1, LANES // ATTN_HEAD_DIM))
    return cos_t, sin_t


def kernel(x, p, norm_mix, w_in, conv_w, a_log, dt_bias, dn_norm, sinks, w_o, norm_mlp, w_up, w_down,
           norm_ple, w_ple_gate, w_ple_proj, norm_final):
    B, S, D = x.shape
    N = B * S
    assert D == D_MODEL and p.shape[0] == 1 and S % PROJ_ROWS == 0 and S % DN_ROWS == 0 and N % TAIL_ROWS == 0
    x2 = x.reshape(N, D)
    p2 = p[0].reshape(N, PLE_DIM)
    params = pltpu.CompilerParams

    w = w_in[0]
    c = np.cumsum((ATTN_Q, ATTN_KV, ATTN_KV, DN_W, DN_W, DN_W, DN_W, DN_HEADS, DN_HEADS)).tolist()
    wq = w[:, :c[0]].astype(BF16)

    def dup(cols):
        h0, h1 = cols[:, :ATTN_HEAD_DIM], cols[:, ATTN_HEAD_DIM:]
        return jnp.concatenate([h0, h0, h1, h1], axis=1).astype(BF16)

    wk = dup(w[:, c[0]:c[1]])
    wv = dup(w[:, c[1]:c[2]])
    wd = w[:, c[2]:c[6]].astype(BF16)
    zpad = jnp.zeros((8 - DN_HEADS, D), F32)
    wba = jnp.concatenate([w[:, c[6]:c[7]].T, zpad, w[:, c[7]:c[8]].T, zpad], axis=0).astype(BF16)
    cos_t, sin_t = _rope_tables(S)

    sp = S // PROJ_ROWS
    row_blk = lambda width: pl.BlockSpec((PROJ_ROWS, width), lambda b, s: (b * sp + s, 0))
    q_r, k_r, v_r, d_r, ba_t = pl.pallas_call(
        _proj_kernel,
        grid=(B, sp),
        in_specs=[row_blk(D), _const_spec((1, D)), _const_spec(wq.shape), _const_spec(wk.shape),
                  _const_spec(wv.shape), _const_spec(wd.shape), _const_spec(wba.shape),
                  pl.BlockSpec((PROJ_ROWS, LANES), lambda b, s: (s, 0)),
                  pl.BlockSpec((PROJ_ROWS, LANES), lambda b, s: (s, 0))],
        out_specs=[row_blk(ATTN_Q), row_blk(2 * ATTN_KV), row_blk(2 * ATTN_KV), row_blk(4 * DN_W),
                   pl.BlockSpec((16, PROJ_ROWS), lambda b, s: (0, b * sp + s))],
        out_shape=[jax.ShapeDtypeStruct((N, ATTN_Q), BF16), jax.ShapeDtypeStruct((N, 2 * ATTN_KV), BF16),
                   jax.ShapeDtypeStruct((N, 2 * ATTN_KV), BF16), jax.ShapeDtypeStruct((N, 4 * DN_W), BF16),
                   jax.ShapeDtypeStruct((16, N), F32)],
        compiler_params=params(dimension_semantics=("parallel", "parallel"), vmem_limit_bytes=VMEM_LIMIT),
        name="proj",
    )(x2, norm_mix[0].reshape(1, D), wq, wk, wv, wd, wba, cos_t, sin_t)

    nb = S // ATTN_BLOCK
    cur = lambda width: pl.BlockSpec((ATTN_BLOCK, width), lambda b, n: (b * nb + n, 0))
    prev = lambda width: pl.BlockSpec((ATTN_BLOCK, width), lambda b, n: (b * nb + jnp.maximum(n - 1, 0), 0))
    attn = pl.pallas_call(
        _attn_kernel,
        grid=(B, nb),
        in_specs=[pl.BlockSpec(memory_space=pltpu.SMEM), cur(ATTN_Q), cur(2 * ATTN_KV), prev(2 * ATTN_KV),
                  cur(2 * ATTN_KV), prev(2 * ATTN_KV)],
        out_specs=cur(ATTN_Q),
        out_shape=jax.ShapeDtypeStruct((N, ATTN_Q), BF16),
        compiler_params=params(dimension_semantics=("parallel", "parallel"), vmem_limit_bytes=VMEM_LIMIT),
        name="attn",
    )(sinks[0].astype(F32), q_r, k_r, k_r, v_r, v_r)

    nt = S // DN_ROWS
    dcol = lambda j: pl.BlockSpec((DN_ROWS, DN_W), lambda b, t: (b * nt + t, j))
    pad8 = lambda v: jnp.concatenate([v.astype(F32), jnp.zeros((8 - DN_HEADS,), F32)]).reshape(8, 1)
    dn = pl.pallas_call(
        _dn_kernel,
        grid=(B, nt),
        in_specs=[dcol(0), dcol(1), dcol(2), dcol(3),
                  pl.BlockSpec((16, DN_ROWS), lambda b, t: (0, b * nt + t)),
                  _const_spec((DN_CONV, 3 * DN_W)), _const_spec((8, 1)), _const_spec((8, 1)),
                  _const_spec((1, DN_HEAD_DIM))],
        out_specs=pl.BlockSpec((DN_ROWS, DN_W), lambda b, t: (b * nt + t, 0)),
        out_shape=jax.ShapeDtypeStruct((N, DN_W), BF16),
        scratch_shapes=[pltpu.VMEM((DN_ROWS + 8, 3 * DN_W), F32),
                        pltpu.VMEM((DN_HEADS, DN_HEAD_DIM, DN_HEAD_DIM), F32)],
        compiler_params=params(dimension_semantics=("parallel", "arbitrary"), vmem_limit_bytes=VMEM_LIMIT),
        name="deltanet",
    )(d_r, d_r, d_r, d_r, ba_t, conv_w[0].astype(F32), pad8(a_log[0]), pad8(dt_bias[0]),
      dn_norm[0].reshape(1, DN_HEAD_DIM).astype(F32))

    nc = D_FF // FF_CHUNK
    wu = w_up[0].reshape(D, nc, FF_CHUNK).transpose(1, 0, 2).astype(BF16)
    wdn = w_down[0].reshape(nc, FF_CHUNK, D).astype(BF16)
    tail_blk = lambda width: pl.BlockSpec((TAIL_ROWS, width), lambda i: (i, 0))
    vec = lambda g: g.reshape(1, D).astype(F32)
    out = pl.pallas_call(
        _tail_kernel,
        grid=(N // TAIL_ROWS,),
        in_specs=[tail_blk(D), tail_blk(ATTN_Q), tail_blk(DN_W), tail_blk(PLE_DIM),
                  _const_spec((D, D)), _const_spec((1, D)), _const_spec(wu.shape), _const_spec(wdn.shape),
                  _const_spec((1, D)), _const_spec((D, D)), _const_spec((PLE_DIM, D)), _const_spec((1, D))],
        out_specs=tail_blk(D),
        out_shape=jax.ShapeDtypeStruct((N, D), F32),
        compiler_params=params(dimension_semantics=("parallel",), vmem_limit_bytes=VMEM_LIMIT),
        name="tail",
    )(x2, attn, dn, p2, w_o[0].astype(BF16), vec(norm_mlp[0]), wu, wdn, vec(norm_ple[0]),
      w_ple_gate[0].astype(BF16), w_ple_proj[0].astype(BF16), vec(norm_final))
    return out.reshape(B, S, D)
```

```python
import functools

import jax
import jax.numpy as jnp
import numpy as np
from jax import lax
from jax.experimental import pallas as pl
from jax.experimental.pallas import tpu as pltpu

F32 = jnp.float32
BF16 = jnp.bfloat16

D_MODEL = 1024
PLE_DIM = 256
ATTN_HEADS = 8
ATTN_KV_HEADS = 2
ATTN_HEAD_DIM = 64
ATTN_BLOCK = 128
ROPE_THETA = 10000.0
DN_HEADS = 4
DN_HEAD_DIM = 128
DN_CONV = 4
DN_CHUNK = 64
D_FF = 4 * D_MODEL
EPS = 1e-6
ATTN_Q = ATTN_HEADS * ATTN_HEAD_DIM
ATTN_KV = ATTN_KV_HEADS * ATTN_HEAD_DIM
DN_W = DN_HEADS * DN_HEAD_DIM

LANES = 128
PAIR = 2 * DN_CHUNK
NEG = -1e30
HALO = 8

PROJ_ROWS = 512
ATTN_ROWS = 256
DN_ROWS = 256
TAIL_ROWS = 512
FF_CHUNK = 1024
VMEM_LIMIT = 56 * 1024 * 1024


def _dot(a, b):
    return jnp.dot(a, b, preferred_element_type=F32)


def _dot_nt(a, b):
    return lax.dot_general(a, b, (((1,), (1,)), ((), ())), preferred_element_type=F32)


def _dot_tn(a, b):
    return lax.dot_general(a, b, (((0,), (0,)), ((), ())), preferred_element_type=F32)


def _rms(x, g):
    return x * lax.rsqrt(jnp.mean(x * x, axis=-1, keepdims=True) + EPS) * g


def _rope(t, cos, sin_signed, first_half):
    width = t.shape[-1]
    reps = width // LANES
    fwd = pltpu.roll(t, ATTN_HEAD_DIM // 2, axis=1)
    bwd = pltpu.roll(t, width - ATTN_HEAD_DIM // 2, axis=1)
    partner = jnp.where(jnp.tile(first_half, (1, reps)), bwd, fwd)
    return t * jnp.tile(cos, (1, reps)) + partner * jnp.tile(sin_signed, (1, reps))


def _silu(x):
    return x / (1.0 + jnp.exp(-x))


def _proj_kernel(x_ref, g_ref, wq_ref, wk_ref, wv_ref, wd_ref, wba_ref, cos_ref, sin_ref, cw_ref,
                 q_out, k_out, v_out, dq_out, dk_out, dv_out, dz_out, ba_out, xbuf):
    rows = x_ref.shape[0]

    @pl.when(pl.program_id(1) == 0)
    def _():
        xbuf[0:HALO, :] = jnp.zeros((HALO, 3 * DN_W), F32)

    u = _rms(x_ref[...], g_ref[...]).astype(BF16)
    cos = cos_ref[...]
    sin_signed = sin_ref[...]
    lane = lax.broadcasted_iota(jnp.int32, cos.shape, 1)
    first_half = (lane % ATTN_HEAD_DIM) < (ATTN_HEAD_DIM // 2)

    q = _rope(_dot(u, wq_ref[...]), cos, sin_signed, first_half)
    q_out[...] = (q * (ATTN_HEAD_DIM ** -0.5)).astype(BF16)
    k = _rope(_dot(u, wk_ref[...]), cos, sin_signed, first_half)
    k_out[...] = k.astype(BF16)
    v_out[...] = _dot(u, wv_ref[...]).astype(BF16)
    for j in range(3):
        sl = slice(j * DN_W, (j + 1) * DN_W)
        xbuf[HALO:HALO + rows, sl] = _dot(u, wd_ref[:, sl])
    dz_out[...] = _dot(u, wd_ref[:, 3 * DN_W:4 * DN_W]).astype(BF16)
    ba_out[...] = _dot_nt(wba_ref[...], u)

    for c in range(3 * DN_HEADS):
        cs = slice(c * DN_HEAD_DIM, (c + 1) * DN_HEAD_DIM)
        acc = xbuf[pl.ds(HALO, rows), cs] * cw_ref[DN_CONV - 1:DN_CONV, cs]
        for j in range(DN_CONV - 1):
            acc = acc + xbuf[pl.ds(HALO - (DN_CONV - 1) + j, rows), cs] * cw_ref[j:j + 1, cs]
        y = _silu(acc)
        part, h = divmod(c, DN_HEADS)
        hs = slice(h * DN_HEAD_DIM, (h + 1) * DN_HEAD_DIM)
        if part == 0:
            y = y * lax.rsqrt(jnp.sum(y * y, axis=-1, keepdims=True) + EPS) * (DN_HEAD_DIM ** -0.5)
            dq_out[:, hs] = y.astype(BF16)
        elif part == 1:
            y = y * lax.rsqrt(jnp.sum(y * y, axis=-1, keepdims=True) + EPS)
            dk_out[:, hs] = y.astype(BF16)
        else:
            dv_out[:, hs] = y.astype(BF16)
    xbuf[0:HALO, :] = xbuf[rows:rows + HALO, :]


def _attn_kernel(sink_ref, q_ref, kc_ref, kp_ref, vc_ref, vp_ref, o_ref):
    n = pl.program_id(1)
    blocks = q_ref.shape[0] // ATTN_BLOCK
    groups = ATTN_HEADS // ATTN_KV_HEADS
    row = lax.broadcasted_iota(jnp.int32, (ATTN_BLOCK, ATTN_BLOCK), 0)
    col = lax.broadcasted_iota(jnp.int32, (ATTN_BLOCK, ATTN_BLOCK), 1)
    upper = col > row
    low = col < ATTN_HEAD_DIM
    zero = jnp.zeros((ATTN_BLOCK, LANES), BF16)

    def blk(ref, r, gs):
        return ref[r * ATTN_BLOCK:(r + 1) * ATTN_BLOCK, gs]

    units = [(r, g) for r in range(blocks) for g in range(ATTN_KV_HEADS)]
    scores, vws = {}, {}
    for r, g in units:
        gs = slice(g * LANES, (g + 1) * LANES)
        k_prev = kp_ref[:, gs] if r == 0 else blk(kc_ref, r - 1, gs)
        v_prev = vp_ref[:, gs] if r == 0 else blk(vc_ref, r - 1, gs)
        kw = jnp.concatenate([k_prev, blk(kc_ref, r, gs)], axis=0)
        vws[(r, g)] = jnp.concatenate([v_prev, blk(vc_ref, r, gs)], axis=0)
        qs = []
        for j in range(groups):
            head = g * groups + j
            q2 = blk(q_ref, r, slice((head // 2) * LANES, (head // 2 + 1) * LANES))
            qs.append(jnp.where(low if head % 2 == 0 else jnp.logical_not(low), q2, zero))
        scores[(r, g)] = _dot_nt(jnp.concatenate(qs, axis=0), kw)
    probs, denoms = {}, {}
    for r, g in units:
        ps = []
        for j in range(groups):
            head = g * groups + j
            s = scores[(r, g)][j * ATTN_BLOCK:(j + 1) * ATTN_BLOCK]
            s_prev = s[:, :ATTN_BLOCK]
            if r == 0:
                s_prev = jnp.where(n > 0, s_prev, NEG)
            s = jnp.where(upper, s_prev, s[:, ATTN_BLOCK:])
            sink = sink_ref[head]
            m = jnp.maximum(jnp.max(s, axis=1, keepdims=True), sink)
            p = jnp.exp(s - m)
            denoms[(r, head)] = jnp.sum(p, axis=1, keepdims=True) + jnp.exp(sink - m)
            ps.append(jnp.concatenate([jnp.where(upper, p, 0.0), jnp.where(upper, 0.0, p)], axis=1).astype(BF16))
        probs[(r, g)] = jnp.concatenate(ps, axis=0)
    for r, g in units:
        o = _dot(probs[(r, g)], vws[(r, g)])
        for jp in range(groups // 2):
            he = g * groups + 2 * jp
            o_even = o[(2 * jp) * ATTN_BLOCK:(2 * jp + 1) * ATTN_BLOCK] / denoms[(r, he)]
            o_odd = o[(2 * jp + 1) * ATTN_BLOCK:(2 * jp + 2) * ATTN_BLOCK] / denoms[(r, he + 1)]
            o_ref[r * ATTN_BLOCK:(r + 1) * ATTN_BLOCK, (he // 2) * LANES:(he // 2 + 1) * LANES] = (
                jnp.where(low, o_even, o_odd).astype(BF16))


def _dn_kernel(dq_ref, dk_ref, dv_ref, dz_ref, ba_ref, alog_ref, dtb_ref, nw_ref, o_ref, state):
    rows = dq_ref.shape[0]

    @pl.when(pl.program_id(1) == 0)
    def _():
        state[...] = jnp.zeros(state.shape, F32)

    beta_all = jax.nn.sigmoid(ba_ref[0:8, :])
    a_all = ba_ref[8:16, :] + dtb_ref[...]
    softplus = jnp.maximum(a_all, 0.0) + jnp.log(1.0 + jnp.exp(-jnp.abs(a_all)))
    g_all = -jnp.exp(alog_ref[...]) * softplus

    ri = lax.broadcasted_iota(jnp.int32, (PAIR, PAIR), 0)
    ci = lax.broadcasted_iota(jnp.int32, (PAIR, PAIR), 1)
    same = (ri // DN_CHUNK) == (ci // DN_CHUNK)
    tril = same & (ci <= ri)
    strict = same & (ci < ri)
    eye = ci == ri
    eye_f = eye.astype(F32)

    heads = range(DN_HEADS)
    pairs = range(rows // PAIR)
    items = [(h, pr) for pr in pairs for h in heads]

    it = {}
    for h, pr in items:
        rs = slice(pr * PAIR, (pr + 1) * PAIR)
        hs = slice(h * DN_HEAD_DIM, (h + 1) * DN_HEAD_DIM)
        d = dict(q=dq_ref[rs, hs], k=dk_ref[rs, hs], v=dv_ref[rs, hs])
        g_row = g_all[h:h + 1, rs]
        d["beta_row"] = beta_all[h:h + 1, rs]
        d["gc_col"] = jnp.sum(jnp.where(tril, g_row, 0.0), axis=1, keepdims=True)
        d["glast_col"] = jnp.sum(jnp.where(same, g_row, 0.0), axis=1, keepdims=True)
        beta_col = jnp.sum(jnp.where(eye, d["beta_row"], 0.0), axis=1, keepdims=True)
        gc_rowb = jnp.transpose(jnp.broadcast_to(d["gc_col"], (PAIR, PAIR)))
        d["eg_row"] = jnp.exp(gc_rowb[0:1, :])
        d["decay"] = jnp.exp(jnp.where(tril, d["gc_col"] - gc_rowb, NEG))
        kq = _dot_nt(jnp.concatenate([d["k"], d["q"]], axis=0), d["k"])
        d["qk"] = kq[PAIR:]
        d["y"] = jnp.where(strict, -(kq[:PAIR] * beta_col * d["decay"]), 0.0)
        d["tm"] = eye_f + d["y"]
        it[(h, pr)] = d
    for d in it.values():
        yb = d["y"].astype(BF16)
        d["y"] = _dot(yb, yb)
    for lvl in range(4):
        for d in it.values():
            yb = d["y"].astype(BF16)
            both = _dot(jnp.concatenate([d["tm"].astype(BF16), yb], axis=0), yb)
            d["tm"] = d["tm"] + both[:PAIR]
            d["y"] = both[PAIR:]
    for d in it.values():
        d["tm"] = d["tm"] + _dot(d["tm"].astype(BF16), d["y"].astype(BF16))
    for d in it.values():
        tm, beta_row = d["tm"], d["beta_row"]
        d["u"] = _dot((tm * beta_row).astype(BF16), d["v"])
        d["w"] = _dot((tm * (beta_row * d["eg_row"])).astype(BF16), d["k"])
        d["a"] = jnp.where(tril, d["qk"] * d["decay"], 0.0).astype(BF16)
        d["qg"] = (d["q"].astype(F32) * jnp.exp(d["gc_col"])).astype(BF16)
        d["kd"] = (d["k"].astype(F32) * jnp.exp(d["glast_col"] - d["gc_col"])).astype(BF16)

    s = [state[h] for h in heads]
    o_parts = [[] for _ in heads]
    for pr in pairs:
        vnew0 = [None] * DN_HEADS
        for c in range(2):
            cr = slice(c * DN_CHUNK, (c + 1) * DN_CHUNK)
            ws = []
            for h in heads:
                d = it[(h, pr)]
                lhs = jnp.concatenate([d["w"][cr].astype(BF16), d["qg"][cr]], axis=0)
                ws.append(_dot(lhs, s[h].astype(BF16)))
            vnew = [it[(h, pr)]["u"][cr] - ws[h][:DN_CHUNK] for h in heads]
            for h in heads:
                d = it[(h, pr)]
                if c == 0:
                    vnew0[h] = vnew[h]
                    vfull = jnp.concatenate([vnew[h], jnp.zeros_like(vnew[h])], axis=0)
                else:
                    vfull = jnp.concatenate([vnew0[h], vnew[h]], axis=0)
                o_parts[h].append(ws[h][DN_CHUNK:] + _dot(d["a"][cr], vfull.astype(BF16)))
            for h in heads:
                d = it[(h, pr)]
                d_last = jnp.exp(d["glast_col"][c * DN_CHUNK:c * DN_CHUNK + 1, :])
                s[h] = s[h] * d_last + _dot_tn(d["kd"][cr], vnew[h].astype(BF16))
    for h in heads:
        hs = slice(h * DN_HEAD_DIM, (h + 1) * DN_HEAD_DIM)
        state[h] = s[h]
        o = jnp.concatenate(o_parts[h], axis=0)
        o = o * lax.rsqrt(jnp.mean(o * o, axis=-1, keepdims=True) + EPS) * nw_ref[...]
        o_ref[:, hs] = (o * _silu(dz_ref[:, hs].astype(F32))).astype(BF16)


def _tail_kernel(x_ref, at_ref, dn_ref, p_ref, wo_ref, gm_ref, wu_ref, wd_ref, gp_ref, wg_ref, wp_ref,
                 gf_ref, o_ref):
    h = x_ref[...] + _dot(at_ref[...], wo_ref[0:ATTN_Q, :]) + _dot(dn_ref[...], wo_ref[ATTN_Q:, :])
    m = _rms(h, gm_ref[...]).astype(BF16)
    acc = jnp.zeros(h.shape, F32)
    for c in range(D_FF // FF_CHUNK):
        up = jnp.maximum(_dot(m, wu_ref[c]), 0.0)
        acc = acc + _dot((up * up).astype(BF16), wd_ref[c])
    h = h + acc
    gate = jax.nn.sigmoid(_dot(_rms(h, gp_ref[...]).astype(BF16), wg_ref[...]))
    h = h + gate * _dot(p_ref[...].astype(BF16), wp_ref[...])
    o_ref[...] = _rms(h, gf_ref[...])


def _const_spec(shape):
    nd = len(shape)
    return pl.BlockSpec(shape, lambda *_: (0,) * nd, pipeline_mode=pl.Buffered(1))


def _rope_tables(seq):
    half = ATTN_HEAD_DIM // 2
    inv = 1.0 / (ROPE_THETA ** (jnp.arange(half, dtype=F32) * (2.0 / ATTN_HEAD_DIM)))
    ang = jnp.arange(seq, dtype=F32)[:, None] * inv[None, :]
    cos, sin = jnp.cos(ang), jnp.sin(ang)
    cos_t = jnp.tile(cos, (1, LANES // half))
    sin_t = jnp.tile(jnp.concatenate([-sin, sin], axis=1), (1, LANES // ATTN_HEAD_DIM))
    return cos_t, sin_t


def kernel(x, p, norm_mix, w_in, conv_w, a_log, dt_bias, dn_norm, sinks, w_o, norm_mlp, w_up, w_down,
           norm_ple, w_ple_gate, w_ple_proj, norm_final):
    B, S, D = x.shape
    N = B * S
    assert D == D_MODEL and p.shape[0] == 1 and S % PROJ_ROWS == 0 and S % DN_ROWS == 0 and N % TAIL_ROWS == 0
    x2 = x.reshape(N, D)
    p2 = p[0].reshape(N, PLE_DIM)
    params = pltpu.CompilerParams

    w = w_in[0]
    c = np.cumsum((ATTN_Q, ATTN_KV, ATTN_KV, DN_W, DN_W, DN_W, DN_W, DN_HEADS, DN_HEADS)).tolist()
    wq = w[:, :c[0]].astype(BF16)

    def dup(cols):
        h0, h1 = cols[:, :ATTN_HEAD_DIM], cols[:, ATTN_HEAD_DIM:]
        return jnp.concatenate([h0, h0, h1, h1], axis=1).astype(BF16)

    wk = dup(w[:, c[0]:c[1]])
    wv = dup(w[:, c[1]:c[2]])
    wd = w[:, c[2]:c[6]].astype(BF16)
    zpad = jnp.zeros((8 - DN_HEADS, D), F32)
    wba = jnp.concatenate([w[:, c[6]:c[7]].T, zpad, w[:, c[7]:c[8]].T, zpad], axis=0).astype(BF16)
    cos_t, sin_t = _rope_tables(S)

    sp = S // PROJ_ROWS
    row_blk = lambda width: pl.BlockSpec((PROJ_ROWS, width), lambda b, s: (b * sp + s, 0))
    bf = lambda width: jax.ShapeDtypeStruct((N, width), BF16)
    q_r, k_r, v_r, dq_r, dk_r, dv_r, dz_r, ba_t = pl.pallas_call(
        _proj_kernel,
        grid=(B, sp),
        in_specs=[row_blk(D), _const_spec((1, D)), _const_spec(wq.shape), _const_spec(wk.shape),
                  _const_spec(wv.shape), _const_spec(wd.shape), _const_spec(wba.shape),
                  pl.BlockSpec((PROJ_ROWS, LANES), lambda b, s: (s, 0)),
                  pl.BlockSpec((PROJ_ROWS, LANES), lambda b, s: (s, 0)),
                  _const_spec((DN_CONV, 3 * DN_W))],
        out_specs=[row_blk(ATTN_Q), row_blk(2 * ATTN_KV), row_blk(2 * ATTN_KV),
                   row_blk(DN_W), row_blk(DN_W), row_blk(DN_W), row_blk(DN_W),
                   pl.BlockSpec((16, PROJ_ROWS), lambda b, s: (0, b * sp + s))],
        out_shape=[bf(ATTN_Q), bf(2 * ATTN_KV), bf(2 * ATTN_KV), bf(DN_W), bf(DN_W), bf(DN_W), bf(DN_W),
                   jax.ShapeDtypeStruct((16, N), F32)],
        scratch_shapes=[pltpu.VMEM((PROJ_ROWS + HALO, 3 * DN_W), F32)],
        compiler_params=params(dimension_semantics=("parallel", "arbitrary"), vmem_limit_bytes=VMEM_LIMIT),
        name="proj",
    )(x2, norm_mix[0].reshape(1, D), wq, wk, wv, wd, wba, cos_t, sin_t, conv_w[0].astype(F32))

    nb = S // ATTN_BLOCK
    na = S // ATTN_ROWS
    per = ATTN_ROWS // ATTN_BLOCK
    cur = lambda width: pl.BlockSpec((ATTN_ROWS, width), lambda b, n: (b * na + n, 0))
    prev = lambda width: pl.BlockSpec((ATTN_BLOCK, width), lambda b, n: (b * nb + jnp.maximum(n * per - 1, 0), 0))
    attn = pl.pallas_call(
        _attn_kernel,
        grid=(B, na),
        in_specs=[pl.BlockSpec(memory_space=pltpu.SMEM), cur(ATTN_Q), cur(2 * ATTN_KV), prev(2 * ATTN_KV),
                  cur(2 * ATTN_KV), prev(2 * ATTN_KV)],
        out_specs=cur(ATTN_Q),
        out_shape=jax.ShapeDtypeStruct((N, ATTN_Q), BF16),
        compiler_params=params(dimension_semantics=("parallel", "parallel"), vmem_limit_bytes=VMEM_LIMIT),
        name="attn",
    )(sinks[0].astype(F32), q_r, k_r, k_r, v_r, v_r)

    nt = S // DN_ROWS
    dblk = pl.BlockSpec((DN_ROWS, DN_W), lambda b, t: (b * nt + t, 0))
    pad8 = lambda v: jnp.concatenate([v.astype(F32), jnp.zeros((8 - DN_HEADS,), F32)]).reshape(8, 1)
    dn = pl.pallas_call(
        _dn_kernel,
        grid=(B, nt),
        in_specs=[dblk, dblk, dblk, dblk,
                  pl.BlockSpec((16, DN_ROWS), lambda b, t: (0, b * nt + t)),
                  _const_spec((8, 1)), _const_spec((8, 1)), _const_spec((1, DN_HEAD_DIM))],
        out_specs=dblk,
        out_shape=jax.ShapeDtypeStruct((N, DN_W), BF16),
        scratch_shapes=[pltpu.VMEM((DN_HEADS, DN_HEAD_DIM, DN_HEAD_DIM), F32)],
        compiler_params=params(dimension_semantics=("parallel", "arbitrary"), vmem_limit_bytes=VMEM_LIMIT),
        name="deltanet",
    )(dq_r, dk_r, dv_r, dz_r, ba_t, pad8(a_log[0]), pad8(dt_bias[0]),
      dn_norm[0].reshape(1, DN_HEAD_DIM).astype(F32))

    nc = D_FF // FF_CHUNK
    wu = w_up[0].reshape(D, nc, FF_CHUNK).transpose(1, 0, 2).astype(BF16)
    wdn = w_down[0].reshape(nc, FF_CHUNK, D).astype(BF16)
    tail_blk = lambda width: pl.BlockSpec((TAIL_ROWS, width), lambda i: (i, 0))
    vec = lambda g: g.reshape(1, D).astype(F32)
    out = pl.pallas_call(
        _tail_kernel,
        grid=(N // TAIL_ROWS,),
        in_specs=[tail_blk(D), tail_blk(ATTN_Q), tail_blk(DN_W), tail_blk(PLE_DIM),
                  _const_spec((D, D)), _const_spec((1, D)), _const_spec(wu.shape), _const_spec(wdn.shape),
                  _const_spec((1, D)), _const_spec((D, D)), _const_spec((PLE_DIM, D)), _const_spec((1, D))],
        out_specs=tail_blk(D),
        out_shape=jax.ShapeDtypeStruct((N, D), F32),
        compiler_params=params(dimension_semantics=("parallel",), vmem_limit_bytes=VMEM_LIMIT),
        name="tail",
    )(x2, attn, dn, p2, w_o[0].astype(BF16), vec(norm_mlp[0]), wu, wdn, vec(norm_ple[0]),
      w_ple_gate[0].astype(BF16), w_ple_proj[0].astype(BF16), vec(norm_final))
    return out.reshape(B, S, D)
```
